```python
import jax, jax.numpy as jnp
from jax import lax
import numpy as np

D_MODEL = 1024
BATCH = 1
SEQ = 16384
DEPTH = 2
DEC_BATCH = 32
DEC_SEQ = 4
PAST_LEN = 16384
PAGE_SIZE = 128

A_GROUPS = 4
A_GROUP_W = 128
A_W = A_GROUPS * A_GROUP_W
A_CHUNK = 128
B_HEADS = 4
B_DK = 64
B_DV = 128
B_QK_W = 2 * B_HEADS * B_DK
B_V_W = B_HEADS * B_DV
B_CONV = 4
B_CHUNK = 128
C_HEADS = 8
C_HD = 64
C_W = C_HEADS * C_HD
C_QBLOCK = 128
N_BRANCH = 3
D_FF = ((8 * D_MODEL // 3 + 255) // 256) * 256
EPS = 1e-6
IN_SIZES = (A_W, A_W, B_QK_W, B_V_W, B_V_W, B_HEADS, B_HEADS, C_W, C_W, C_W, C_HEADS, N_BRANCH * D_MODEL)
P_IN = sum(IN_SIZES)

kernel_name = "gated_hybrid_gmlp_mlstm_fox_step"

F32 = jnp.float32


def rmsnorm(x, g):
    x32 = x.astype(F32)
    y = x32 * lax.rsqrt(jnp.mean(x32 * x32, axis=-1, keepdims=True) + EPS)
    return (y * g.astype(F32)).astype(x.dtype)


def split_in(z):
    idx = []
    acc = 0
    for s in IN_SIZES[:-1]:
        acc += s
        idx.append(acc)
    return jnp.split(z, idx, axis=-1)


def chunk_mlp(u, v, w_s, b_s, g_v):
    bsz, t, _ = u.shape
    L = min(t, A_CHUNK)
    vn = rmsnorm(v, g_v)
    mask = jnp.tril(jnp.ones((L, L), dtype=bool))
    w = jnp.where(mask, w_s[:, :L, :L], 0).astype(vn.dtype)
    vc = vn.reshape(bsz, t // L, L, A_GROUPS, A_GROUP_W)
    mixed = jnp.einsum('gts,bnsgc->bntgc', w, vc) + b_s[:, :L].T[:, :, None].astype(vn.dtype)
    return u * mixed.reshape(bsz, t, A_W), vn


def short_conv(x, buf, w, bias):
    full = jnp.concatenate([buf.astype(x.dtype), x], axis=1)
    t = x.shape[1]
    y = full[:, 0:t] * w[0]
    for j in range(1, B_CONV):
        y = y + full[:, j:j + t] * w[j]
    return jax.nn.silu(y + bias), full[:, -(B_CONV - 1):]


def mlstm_chunk(carry, xs):
    c, n, m = carry
    q, k, v, ig, lf = xs
    L = q.shape[1]
    b = jnp.cumsum(lf, axis=1)
    causal = jnp.tril(jnp.ones((L, L), dtype=bool))[None, :, :, None]
    d = jnp.where(causal, b[:, :, None, :] - b[:, None, :, :] + ig[:, None, :, :], -jnp.inf)
    inter = b + m[:, None, :]
    m_t = jnp.maximum(inter, jnp.max(d, axis=2))
    wts = jnp.exp(d - m_t[:, :, None, :]) * jnp.einsum('bthd,bshd->btsh', q, k)
    dec = jnp.exp(inter - m_t)
    num = jnp.einsum('btsh,bshv->bthv', wts, v) + dec[..., None] * jnp.einsum('bthd,bhdv->bthv', q, c)
    den = jnp.sum(wts, axis=2) + dec * jnp.einsum('bthd,bhd->bth', q, n)
    h = num / jnp.maximum(jnp.abs(den), jnp.exp(-m_t))[..., None]
    m_new = m_t[:, -1]
    g = jnp.exp(b[:, -1:, :] - b + ig - m_new[:, None, :])
    carry_dec = jnp.exp(b[:, -1] + m - m_new)
    c_new = carry_dec[..., None, None] * c + jnp.einsum('bsh,bshd,bshv->bhdv', g, k, v)
    n_new = carry_dec[..., None] * n + jnp.einsum('bsh,bshd->bhd', g, k)
    return (c_new, n_new, m_new), h


def mlstm_scan(q, k, v, ig, lf, state):
    bsz, t = q.shape[:2]
    L = min(t, B_CHUNK)

    def to_chunks(a):
        return jnp.moveaxis(a.astype(F32).reshape((bsz, t // L, L) + a.shape[2:]), 1, 0)

    state, h = lax.scan(mlstm_chunk, state, (to_chunks(q), to_chunks(k), to_chunks(v), to_chunks(ig), to_chunks(lf)))
    h = jnp.moveaxis(h, 0, 1).reshape(bsz, t, B_V_W)
    return h, state


def fox_prompt(q, k, v, lf):
    bsz, t = q.shape[:2]
    F = jnp.cumsum(lf, axis=1)
    F_k = jnp.moveaxis(F, 1, 2)[:, :, None, :]
    kpos = jnp.arange(t)
    scale = C_HD ** -0.5

    def block(i):
        s0 = i * C_QBLOCK
        qb = lax.dynamic_slice_in_dim(q, s0, C_QBLOCK, axis=1)
        Fq = lax.dynamic_slice_in_dim(F, s0, C_QBLOCK, axis=1)
        logits = jnp.einsum('bqhd,bkhd->bhqk', qb, k).astype(F32) * scale + jnp.moveaxis(Fq, 1, 2)[..., None] - F_k
        qpos = s0 + jnp.arange(C_QBLOCK)
        logits = jnp.where(kpos[None, :] <= qpos[:, None], logits, -jnp.inf)
        p = jax.nn.softmax(logits, axis=-1).astype(v.dtype)
        return jnp.einsum('bhqk,bkhd->bqhd', p, v)

    out = lax.map(block, jnp.arange(t // C_QBLOCK))
    return jnp.moveaxis(out, 0, 1).reshape(bsz, t, C_W)


def fox_sample(q, k, v, lf, kp, vp, lfp):
    db, t = q.shape[:2]
    past = kp.shape[1] * kp.shape[2]
    k_all = jnp.concatenate([kp.reshape(db, past, C_HEADS, C_HD).astype(k.dtype), k], axis=1)
    v_all = jnp.concatenate([vp.reshape(db, past, C_HEADS, C_HD).astype(v.dtype), v], axis=1)
    lf_all = jnp.concatenate([lfp.reshape(db, past, C_HEADS).astype(F32), lf], axis=1)
    F = jnp.cumsum(lf_all, axis=1)
    Fq = F[:, past:]
    logits = (jnp.einsum('bqhd,bkhd->bhqk', q, k_all).astype(F32) * (C_HD ** -0.5)
              + jnp.moveaxis(Fq, 1, 2)[..., None] - jnp.moveaxis(F, 1, 2)[:, :, None, :])
    mask = jnp.arange(past + t)[None, :] <= past + jnp.arange(t)[:, None]
    logits = jnp.where(mask, logits, -jnp.inf)
    p = jax.nn.softmax(logits, axis=-1).astype(v.dtype)
    return jnp.einsum('bhqk,bkhd->bqhd', p, v_all).reshape(db, t, C_W)


def layer(x, prm, fox_fn, conv_buf, mstate):
    bsz, t, _ = x.shape
    h = rmsnorm(x, prm['g_pre_mix'])
    z = h @ prm['w_in']
    a_u, a_v, b_qk, b_v, b_o, b_i, b_f, c_q, c_k, c_v, c_f, z_g = split_in(z)
    y_a, a_rows = chunk_mlp(a_u, a_v, prm['a_w_s'], prm['a_b_s'], prm['a_g_v'])
    qk, conv_new = short_conv(b_qk, conv_buf, prm['b_conv_w'], prm['b_conv_b'])
    bq = qk[..., :B_HEADS * B_DK].reshape(bsz, t, B_HEADS, B_DK)
    bk = qk[..., B_HEADS * B_DK:].reshape(bsz, t, B_HEADS, B_DK) * (B_DK ** -0.5)
    bv = b_v.reshape(bsz, t, B_HEADS, B_DV)
    ig = b_i.astype(F32) + prm['b_b_i'].astype(F32)
    blf = jax.nn.log_sigmoid(b_f.astype(F32) + prm['b_b_f'].astype(F32))
    bh, mstate_new = mlstm_scan(bq, bk, bv, ig, blf, mstate)
    y_b = (jax.nn.sigmoid(b_o.astype(F32)) * bh).astype(x.dtype)
    cq = c_q.reshape(bsz, t, C_HEADS, C_HD)
    ck = c_k.reshape(bsz, t, C_HEADS, C_HD)
    cv = c_v.reshape(bsz, t, C_HEADS, C_HD)
    clf = jax.nn.log_sigmoid(c_f.astype(F32) + prm['c_b_f'].astype(F32))
    y_c = fox_fn(cq, ck, cv, clf)
    gates = jax.nn.sigmoid(z_g).reshape(bsz, t, N_BRANCH, D_MODEL)
    merged = (gates[:, :, 0] * (y_a @ prm['w_out_a']) + gates[:, :, 1] * (y_b @ prm['w_out_b'])
              + gates[:, :, 2] * (y_c @ prm['w_out_c']))
    x = x + rmsnorm(merged @ prm['w_o'], prm['g_post_mix'])
    h = rmsnorm(x, prm['g_pre_ffn'])
    f = (jax.nn.silu(h @ prm['w_gate']) * (h @ prm['w_up'])) @ prm['w_down']
    x = x + rmsnorm(f, prm['g_post_ffn'])
    return x, (ck, cv, clf, conv_new, mstate_new[0], mstate_new[1], mstate_new[2], a_rows)


def setup_inputs(seed: int = 0) -> dict:
    key = jax.random.key(seed)
    ks = jax.random.split(key, 40)
    n_pages = PAST_LEN // PAGE_SIZE
    n_used = DEC_BATCH * n_pages
    n_pool = n_used + max(1, n_used // 4)

    def nrm(k, shape, s):
        return s * jax.random.normal(k, shape, F32)

    d = {}
    d['x_prompt'] = nrm(ks[0], (BATCH, SEQ, D_MODEL), 1.0)
    d['x_sample'] = nrm(ks[1], (DEC_BATCH, DEC_SEQ, D_MODEL), 1.0)
    d['cache_k'] = nrm(ks[2], (DEPTH, n_pool, PAGE_SIZE, C_HEADS, C_HD), 1.0)
    d['cache_v'] = nrm(ks[3], (DEPTH, n_pool, PAGE_SIZE, C_HEADS, C_HD), 1.0)
    d['cache_logf'] = jax.nn.log_sigmoid(nrm(ks[4], (DEPTH, n_pool, PAGE_SIZE, C_HEADS), 1.0) + 3.0)
    d['page_table'] = jax.random.permutation(ks[5], n_pool)[:n_used].reshape(DEC_BATCH, n_pages).astype(jnp.int32)
    d['state_mlstm_conv'] = nrm(ks[6], (DEPTH, DEC_BATCH, B_CONV - 1, B_QK_W), 1.0)
    d['state_mlstm_c'] = nrm(ks[7], (DEPTH, DEC_BATCH, B_HEADS, B_DK, B_DV), 0.1)
    d['state_mlstm_n'] = nrm(ks[8], (DEPTH, DEC_BATCH, B_HEADS, B_DK), 0.1)
    d['state_mlstm_m'] = nrm(ks[9], (DEPTH, DEC_BATCH, B_HEADS), 1.0)
    d['g_pre_mix'] = 1.0 + nrm(ks[10], (DEPTH, D_MODEL), 0.05)
    d['w_in'] = nrm(ks[11], (DEPTH, D_MODEL, P_IN), D_MODEL ** -0.5)
    d['a_w_s'] = nrm(ks[12], (DEPTH, A_GROUPS, A_CHUNK, A_CHUNK), 0.05)
    d['a_b_s'] = 1.0 + nrm(ks[13], (DEPTH, A_GROUPS, A_CHUNK), 0.1)
    d['a_g_v'] = 1.0 + nrm(ks[14], (DEPTH, A_W), 0.05)
    d['b_conv_w'] = nrm(ks[15], (DEPTH, B_CONV, B_QK_W), B_CONV ** -0.5)
    d['b_conv_b'] = nrm(ks[16], (DEPTH, B_QK_W), 0.02)
    d['b_b_i'] = nrm(ks[17], (DEPTH, B_HEADS), 0.1)
    d['b_b_f'] = jax.random.uniform(ks[18], (DEPTH, B_HEADS), F32, 3.0, 6.0)
    d['c_b_f'] = jax.random.uniform(ks[19], (DEPTH, C_HEADS), F32, 1.0, 5.0)
    d['w_out_a'] = nrm(ks[20], (DEPTH, A_W, D_MODEL), A_W ** -0.5)
    d['w_out_b'] = nrm(ks[21], (DEPTH, B_V_W, D_MODEL), B_V_W ** -0.5)
    d['w_out_c'] = nrm(ks[22], (DEPTH, C_W, D_MODEL), C_W ** -0.5)
    d['w_o'] = nrm(ks[23], (DEPTH, D_MODEL, D_MODEL), D_MODEL ** -0.5)
    d['g_post_mix'] = 1.0 + nrm(ks[24], (DEPTH, D_MODEL), 0.05)
    d['g_pre_ffn'] = 1.0 + nrm(ks[25], (DEPTH, D_MODEL), 0.05)
    d['w_gate'] = nrm(ks[26], (DEPTH, D_MODEL, D_FF), D_MODEL ** -0.5)
    d['w_up'] = nrm(ks[27], (DEPTH, D_MODEL, D_FF), D_MODEL ** -0.5)
    d['w_down'] = nrm(ks[28], (DEPTH, D_FF, D_MODEL), D_FF ** -0.5)
    d['g_post_ffn'] = 1.0 + nrm(ks[29], (DEPTH, D_MODEL), 0.05)
    return d


def _stack(outs, i):
    return jnp.stack([o[i] for o in outs], axis=0)


def reference(x_prompt, x_sample, cache_k, cache_v, cache_logf, page_table, state_mlstm_conv, state_mlstm_c,
              state_mlstm_n, state_mlstm_m, g_pre_mix, w_in, a_w_s, a_b_s, a_g_v, b_conv_w, b_conv_b, b_b_i,
              b_b_f, c_b_f, w_out_a, w_out_b, w_out_c, w_o, g_post_mix, g_pre_ffn, w_gate, w_up, w_down,
              g_post_ffn):
    xp = x_prompt
    xs = x_sample
    bsz_p = xp.shape[0]
    outs_p = []
    outs_s = []
    for l in range(DEPTH):
        prm = {'g_pre_mix': g_pre_mix[l], 'w_in': w_in[l], 'a_w_s': a_w_s[l], 'a_b_s': a_b_s[l],
               'a_g_v': a_g_v[l], 'b_conv_w': b_conv_w[l], 'b_conv_b': b_conv_b[l], 'b_b_i': b_b_i[l],
               'b_b_f': b_b_f[l], 'c_b_f': c_b_f[l], 'w_out_a': w_out_a[l], 'w_out_b': w_out_b[l],
               'w_out_c': w_out_c[l], 'w_o': w_o[l], 'g_post_mix': g_post_mix[l], 'g_pre_ffn': g_pre_ffn[l],
               'w_gate': w_gate[l], 'w_up': w_up[l], 'w_down': w_down[l], 'g_post_ffn': g_post_ffn[l]}
        conv0 = jnp.zeros((bsz_p, B_CONV - 1, B_QK_W), xp.dtype)
        mstate0 = (jnp.zeros((bsz_p, B_HEADS, B_DK, B_DV), F32), jnp.zeros((bsz_p, B_HEADS, B_DK), F32),
                   jnp.zeros((bsz_p, B_HEADS), F32))
        xp, st_p = layer(xp, prm, fox_prompt, conv0, mstate0)
        fox_s = lambda q, k, v, lf, l=l: fox_sample(q, k, v, lf, cache_k[l, page_table], cache_v[l, page_table],
                                                   cache_logf[l, page_table])
        mstate_s = (state_mlstm_c[l].astype(F32), state_mlstm_n[l].astype(F32), state_mlstm_m[l].astype(F32))
        xs, st_s = layer(xs, prm, fox_s, state_mlstm_conv[l], mstate_s)
        outs_p.append(st_p)
        outs_s.append(st_s)
    return (xp, xs,
            _stack(outs_p, 0), _stack(outs_p, 1), _stack(outs_p, 2),
            _stack(outs_p, 3), _stack(outs_p, 4), _stack(outs_p, 5), _stack(outs_p, 6),
            _stack(outs_s, 0), _stack(outs_s, 1), _stack(outs_s, 2),
            _stack(outs_s, 3), _stack(outs_s, 4), _stack(outs_s, 5), _stack(outs_s, 6),
            _stack(outs_s, 7))
```

```python
import functools

import jax
import jax.numpy as jnp
from jax import lax
from jax.experimental import pallas as pl
from jax.experimental.pallas import tpu as pltpu

F32 = jnp.float32
BF16 = jnp.bfloat16
EPS = 1e-6
HIGHEST = lax.Precision.HIGHEST

LANES = 128
CHUNK = 128
A_GROUPS = 4
B_HEADS = 4
B_DK = 64
B_DV = 128
B_CONV = 4
C_HEADS = 8
C_HD = 64
WIDTH = 512
SMALL = 128
VMEM_LIMIT = 56 * 1024 * 1024

NT_DIMS = (((1,), (1,)), ((), ()))


def _cparams(sem, vmem=VMEM_LIMIT):
    return pltpu.CompilerParams(dimension_semantics=sem, vmem_limit_bytes=vmem)


def _const_spec(shape):
    nd = len(shape)
    return pl.BlockSpec(shape, lambda *_: (0,) * nd, pipeline_mode=pl.Buffered(1))


def _rms(x):
    return x * lax.rsqrt(jnp.mean(x * x, axis=-1, keepdims=True) + EPS)


def _log_sigmoid(x):
    return jnp.minimum(x, 0.0) - jnp.log1p(jnp.exp(-jnp.abs(x)))


def _inproj_kernel(x_ref, g_ref, w_ref, zg_ref, au_ref, av_ref, bqk_ref, bv_ref, bo_ref, ck_ref, cv_ref,
                   sm_ref, qs_ref, kb_ref, vb_ref):
    h = (_rms(x_ref[...]) * g_ref[...]).astype(BF16)

    def mm(c0, n):
        return jnp.dot(h, w_ref[:, c0:c0 + n], preferred_element_type=F32)

    au_ref[...] = mm(0 * WIDTH, WIDTH)
    av_ref[...] = mm(1 * WIDTH, WIDTH)
    bqk_ref[...] = mm(2 * WIDTH, WIDTH)
    bv_ref[...] = mm(3 * WIDTH, WIDTH)
    bo_ref[...] = mm(4 * WIDTH, WIDTH)
    qs_ref[...] = (mm(5 * WIDTH, WIDTH) * (C_HD ** -0.5)).astype(BF16)
    ck = mm(6 * WIDTH, WIDTH)
    ck_ref[...] = ck
    kb_ref[...] = ck.astype(BF16)
    cv = mm(7 * WIDTH, WIDTH)
    cv_ref[...] = cv
    vb_ref[...] = cv.astype(BF16)
    n_gate = zg_ref.shape[1] // WIDTH
    for j in range(n_gate):
        zg_ref[:, j * WIDTH:(j + 1) * WIDTH] = mm((8 + j) * WIDTH, WIDTH)
    sm_ref[...] = mm((8 + n_gate) * WIDTH, SMALL)


def _inproj(x, g, w):
    rows, d = x.shape
    tm = 256 if rows % 256 == 0 else 128
    n_gate = w.shape[1] - 8 * WIDTH - SMALL
    row = lambda n: pl.BlockSpec((tm, n), lambda i: (i, 0))
    f32 = lambda n: jax.ShapeDtypeStruct((rows, n), F32)
    b16 = lambda n: jax.ShapeDtypeStruct((rows, n), BF16)
    return pl.pallas_call(
        _inproj_kernel,
        grid=(rows // tm,),
        in_specs=[row(d), _const_spec((1, d)), _const_spec(w.shape)],
        out_specs=[row(n_gate)] + [row(WIDTH)] * 7 + [row(SMALL)] + [row(WIDTH)] * 3,
        out_shape=[f32(n_gate)] + [f32(WIDTH)] * 7 + [f32(SMALL)] + [b16(WIDTH)] * 3,
        compiler_params=_cparams(("parallel",)),
        name="inproj",
    )(x, g, w)


def _mixa_kernel(u_ref, v_ref, gv_ref, w_ref, bias_ref, ya_ref, *vn_refs, n_chunks):
    vn = _rms(v_ref[...]) * gv_ref[...]
    if vn_refs:
        vn_refs[0][...] = vn
    vb = vn.astype(BF16)
    for c in range(n_chunks):
        r = slice(c * CHUNK, (c + 1) * CHUNK)
        for g in range(A_GROUPS):
            cs = slice(g * LANES, (g + 1) * LANES)
            mixed = jnp.dot(w_ref[g], vb[r, cs], preferred_element_type=F32) + bias_ref[:, cs]
            ya_ref[r, cs] = (u_ref[r, cs] * mixed).astype(BF16)


def _mixa(u, v, gv, w, bias, emit_vn):
    rows = u.shape[0]
    tm = 512 if rows % 512 == 0 else CHUNK
    row = pl.BlockSpec((tm, WIDTH), lambda i: (i, 0))
    out_shape = [jax.ShapeDtypeStruct((rows, WIDTH), BF16)]
    out_specs = [row]
    if emit_vn:
        out_shape.append(jax.ShapeDtypeStruct((rows, WIDTH), F32))
        out_specs.append(row)
    return pl.pallas_call(
        functools.partial(_mixa_kernel, n_chunks=tm // CHUNK),
        grid=(rows // tm,),
        in_specs=[row, row, _const_spec((1, WIDTH)), _const_spec(w.shape), _const_spec(bias.shape)],
        out_specs=out_specs,
        out_shape=out_shape,
        compiler_params=_cparams(("parallel",)),
        name="mixer_a",
    )(u, v, gv, w, bias)


def _gates_kernel(sm_ref, bias_ref, g_ref, cb_ref, fg_ref, gt_ref, cbt_ref, fgt_ref, carry_ref, *, tb):
    @pl.when(pl.program_id(0) == 0)
    def _():
        carry_ref[...] = jnp.zeros_like(carry_ref)

    x = sm_ref[...] + bias_ref[...]
    lane = lax.broadcasted_iota(jnp.int32, x.shape, 1)
    g = jnp.where(lane < B_HEADS, x, _log_sigmoid(x))
    r = lax.broadcasted_iota(jnp.int32, (tb, tb), 0)
    c = lax.broadcasted_iota(jnp.int32, (tb, tb), 1)
    tri = c <= r
    tri_all = jnp.where(tri, 1.0, 0.0).astype(F32)
    tri_chunk = jnp.where(tri & ((r // CHUNK) == (c // CHUNK)), 1.0, 0.0).astype(F32)
    fg = jnp.dot(tri_all, g, precision=HIGHEST, preferred_element_type=F32) + carry_ref[...]
    cb = jnp.dot(tri_chunk, g, precision=HIGHEST, preferred_element_type=F32)
    carry_ref[...] = fg[tb - 1:tb, :]
    g_ref[...] = g
    cb_ref[...] = cb
    fg_ref[...] = fg
    gt_ref[...] = g.T[0:16, :]
    cbt_ref[...] = cb.T[0:16, :]
    fgt_ref[...] = fg.T[0:16, :]


def _gates(small, bias):
    rows = small.shape[0]
    tb = 512 if rows % 512 == 0 else CHUNK
    col = pl.BlockSpec((tb, SMALL), lambda i: (i, 0))
    rowt = pl.BlockSpec((16, tb), lambda i: (0, i))
    return pl.pallas_call(
        functools.partial(_gates_kernel, tb=tb),
        grid=(rows // tb,),
        in_specs=[col, _const_spec((1, SMALL))],
        out_specs=[col, col, col, rowt, rowt, rowt],
        out_shape=[jax.ShapeDtypeStruct((rows, SMALL), F32)] * 3 + [jax.ShapeDtypeStruct((16, rows), F32)] * 3,
        scratch_shapes=[pltpu.VMEM((1, SMALL), F32)],
        compiler_params=_cparams(("arbitrary",)),
        name="gates",
    )(small, bias)


def _mlstm_kernel(*refs, valid, scan):
    if scan:
        (qk_ref, v_ref, o_ref, g_ref, cb_ref, gt_ref, cbt_ref, cw_ref, cbias_ref,
         yb_ref, cn_ref, m_ref, xbuf) = refs
    else:
        (qk_ref, v_ref, o_ref, g_ref, cb_ref, gt_ref, cbt_ref, cw_ref, cbias_ref, conv_in, cn_in, m_in,
         yb_ref, cn_ref, m_ref, xbuf) = refs
    L = CHUNK
    if scan:
        @pl.when(pl.program_id(0) == 0)
        def _():
            xbuf[0:8, :] = jnp.zeros((8, WIDTH), F32)
            cn_ref[...] = jnp.zeros_like(cn_ref)
            m_ref[...] = jnp.zeros_like(m_ref)
    else:
        xbuf[0:8, :] = conv_in[...]
        cn_ref[...] = cn_in[...]
        m_ref[...] = m_in[...]

    xbuf[8:8 + L, :] = qk_ref[...]
    y = xbuf[5:5 + L, :] * cw_ref[0:1, :]
    for j in range(1, B_CONV):
        y = y + xbuf[5 + j:5 + j + L, :] * cw_ref[j:j + 1, :]
    y = y + cbias_ref[...]
    qk = y * jax.nn.sigmoid(y)
    if scan:
        xbuf[0:8, :] = xbuf[L:L + 8, :]

    lane = lax.broadcasted_iota(jnp.int32, (L, LANES), 1)
    row = lax.broadcasted_iota(jnp.int32, (L, L), 0)
    col = lax.broadcasted_iota(jnp.int32, (L, L), 1)
    causal = col <= row
    ones_col = jnp.where(lane == 0, 1.0, 0.0).astype(BF16)
    rowc = lax.broadcasted_iota(jnp.int32, (L, 1), 0)

    for j in range(B_HEADS // 2):
        q_pair = qk[:, j * LANES:(j + 1) * LANES]
        k_pair = qk[:, (2 + j) * LANES:(3 + j) * LANES] * (B_DK ** -0.5)
        kb = k_pair.astype(BF16)
        cn = cn_ref[j]
        cnb = cn.astype(BF16)
        for sub in range(2):
            h = 2 * j + sub
            in_head = (lane >= sub * B_DK) & (lane < (sub + 1) * B_DK)
            qm = jnp.where(in_head, q_pair, 0.0).astype(BF16)
            s = lax.dot_general(qm, kb, NT_DIMS, preferred_element_type=F32)
            bcol = cb_ref[:, 4 + h:5 + h]
            brow = cbt_ref[4 + h:5 + h, :]
            igrow = gt_ref[h:h + 1, :]
            igcol = g_ref[:, h:h + 1]
            d = jnp.where(causal, bcol - brow + igrow, -jnp.inf)
            m_old = m_ref[0:1, h:h + 1]
            inter = bcol + m_old
            m_t = jnp.maximum(inter, jnp.max(d, axis=1, keepdims=True))
            wts = jnp.exp(d - m_t) * s
            dec = jnp.exp(inter - m_t)
            vaug = jnp.concatenate([v_ref[:, h * B_DV:(h + 1) * B_DV].astype(BF16), ones_col], axis=1)
            r = (jnp.dot(wts.astype(BF16), vaug, preferred_element_type=F32)
                 + dec * jnp.dot(qm, cnb, preferred_element_type=F32))
            num = r[:, :B_DV]
            den = r[:, B_DV:B_DV + 1]
            hout = num / jnp.maximum(jnp.abs(den), jnp.exp(-m_t))
            gate = jax.nn.sigmoid(o_ref[:, h * B_DV:(h + 1) * B_DV])
            yb_ref[:, h * B_DV:(h + 1) * B_DV] = (gate * hout).astype(BF16)

            b_last = bcol[valid - 1:valid, :]
            m_new = m_t[valid - 1:valid, :]
            gcol = jnp.exp(b_last - bcol + igcol - m_new)
            if valid < L:
                gcol = jnp.where(rowc < valid, gcol, 0.0)
            cdec = jnp.exp(b_last + m_old - m_new)
            kgt = (k_pair * gcol).T.astype(BF16)
            u = jnp.dot(kgt, vaug, preferred_element_type=F32)
            rs = slice(sub * B_DK, (sub + 1) * B_DK)
            cn_ref[j, rs, :] = cdec * cn[rs, :] + u[rs, :]
            m_ref[0:1, h:h + 1] = m_new


def _mlstm_scan(bqk, bv, bo, g, cb, gt, cbt, cw, cbias):
    rows = bqk.shape[0]
    n = rows // CHUNK
    wide = pl.BlockSpec((CHUNK, WIDTH), lambda i: (i, 0))
    colg = pl.BlockSpec((CHUNK, SMALL), lambda i: (i, 0))
    rowg = pl.BlockSpec((16, CHUNK), lambda i: (0, i))
    return pl.pallas_call(
        functools.partial(_mlstm_kernel, valid=CHUNK, scan=True),
        grid=(n,),
        in_specs=[wide, wide, wide, colg, colg, rowg, rowg, _const_spec(cw.shape), _const_spec(cbias.shape)],
        out_specs=[wide, pl.BlockSpec((2, LANES, 2 * LANES), lambda i: (0, 0, 0)),
                   pl.BlockSpec((8, LANES), lambda i: (0, 0))],
        out_shape=[jax.ShapeDtypeStruct((rows, WIDTH), BF16),
                   jax.ShapeDtypeStruct((2, LANES, 2 * LANES), F32),
                   jax.ShapeDtypeStruct((8, LANES), F32)],
        scratch_shapes=[pltpu.VMEM((CHUNK + 8, WIDTH), F32)],
        compiler_params=_cparams(("arbitrary",)),
        name="mlstm_prompt",
    )(bqk, bv, bo, g, cb, gt, cbt, cw, cbias)


def _mlstm_batch(bqk, bv, bo, g, cb, gt, cbt, cw, cbias, conv_in, cn_in, m_in, valid):
    rows = bqk.shape[0]
    n = rows // CHUNK
    wide = pl.BlockSpec((CHUNK, WIDTH), lambda i: (i, 0))
    colg = pl.BlockSpec((CHUNK, SMALL), lambda i: (i, 0))
    rowg = pl.BlockSpec((16, CHUNK), lambda i: (0, i))
    conv_spec = pl.BlockSpec((None, 8, WIDTH), lambda i: (i, 0, 0))
    cn_spec = pl.BlockSpec((None, 2, LANES, 2 * LANES), lambda i: (i, 0, 0, 0))
    m_spec = pl.BlockSpec((None, 8, LANES), lambda i: (i, 0, 0))
    return pl.pallas_call(
        functools.partial(_mlstm_kernel, valid=valid, scan=False),
        grid=(n,),
        in_specs=[wide, wide, wide, colg, colg, rowg, rowg, _const_spec(cw.shape), _const_spec(cbias.shape),
                  conv_spec, cn_spec, m_spec],
        out_specs=[wide, cn_spec, m_spec],
        out_shape=[jax.ShapeDtypeStruct((rows, WIDTH), BF16),
                   jax.ShapeDtypeStruct((n, 2, LANES, 2 * LANES), F32),
                   jax.ShapeDtypeStruct((n, 8, LANES), F32)],
        scratch_shapes=[pltpu.VMEM((CHUNK + 8, WIDTH), F32)],
        compiler_params=_cparams(("parallel",)),
        name="mlstm_sample",
    )(bqk, bv, bo, g, cb, gt, cbt, cw, cbias, conv_in, cn_in, m_in)


def _fox_prompt_kernel(qi_tab, ki_tab, q_ref, k_ref, v_ref, fc_ref, ft_ref, o_ref, m_scr, l_scr, acc_scr, *, tq):
    s_id = pl.program_id(0)
    qi = qi_tab[s_id]
    ki = ki_tab[s_id]

    @pl.when(ki == 0)
    def _():
        m_scr[...] = jnp.full_like(m_scr, -jnp.inf)
        l_scr[...] = jnp.zeros_like(l_scr)
        acc_scr[...] = jnp.zeros_like(acc_scr)

    lane = lax.broadcasted_iota(jnp.int32, (tq, LANES), 1)
    low = lane < C_HD

    def step(masked):
        if masked:
            row = lax.broadcasted_iota(jnp.int32, (tq, tq), 0)
            col = lax.broadcasted_iota(jnp.int32, (tq, tq), 1)
            keep = col <= row
        for j in range(C_HEADS // 2):
            ls = slice(j * LANES, (j + 1) * LANES)
            q_pair = q_ref[:, ls]
            k_pair = k_ref[:, ls]
            v_pair = v_ref[:, ls]
            upd = []
            for sub in range(2):
                h = 2 * j + sub
                qm = jnp.where(low if sub == 0 else ~low, q_pair, jnp.zeros_like(q_pair))
                s = lax.dot_general(qm, k_pair, NT_DIMS, preferred_element_type=F32)
                s = s + fc_ref[:, 8 + h:9 + h] - ft_ref[8 + h:9 + h, :]
                if masked:
                    s = jnp.where(keep, s, -jnp.inf)
                m_old = m_scr[h]
                m_new = jnp.maximum(m_old, jnp.max(s, axis=1, keepdims=True))
                alpha = jnp.exp(m_old - m_new)
                p = jnp.exp(s - m_new)
                l_scr[h] = alpha * l_scr[h] + jnp.sum(p, axis=1, keepdims=True)
                m_scr[h] = m_new
                pv = jnp.dot(p.astype(BF16), v_pair, preferred_element_type=F32)
                upd.append(alpha * acc_scr[:, ls] + pv)
            acc_scr[:, ls] = jnp.where(low, upd[0], upd[1])

    @pl.when(ki < qi)
    def _():
        step(False)

    @pl.when(ki == qi)
    def _():
        step(True)
        for j in range(C_HEADS // 2):
            ls = slice(j * LANES, (j + 1) * LANES)
            linv = jnp.where(low, 1.0 / l_scr[2 * j], 1.0 / l_scr[2 * j + 1])
            o_ref[:, ls] = (acc_scr[:, ls] * linv).astype(BF16)


def _fox_prompt(qs, kb, vb, fg, fgt):
    t = qs.shape[0]
    tq = min(512, t)
    nq = t // tq
    qi_tab = jnp.asarray([qi for qi in range(nq) for _ in range(qi + 1)], jnp.int32)
    ki_tab = jnp.asarray([ki for qi in range(nq) for ki in range(qi + 1)], jnp.int32)
    qspec = pl.BlockSpec((tq, WIDTH), lambda s, qt, kt: (qt[s], 0))
    kspec = pl.BlockSpec((tq, WIDTH), lambda s, qt, kt: (kt[s], 0))
    grid_spec = pltpu.PrefetchScalarGridSpec(
        num_scalar_prefetch=2,
        grid=(qi_tab.shape[0],),
        in_specs=[qspec, kspec, kspec,
                  pl.BlockSpec((tq, SMALL), lambda s, qt, kt: (qt[s], 0)),
                  pl.BlockSpec((16, tq), lambda s, qt, kt: (0, kt[s]))],
        out_specs=qspec,
        scratch_shapes=[pltpu.VMEM((C_HEADS, tq, 1), F32), pltpu.VMEM((C_HEADS, tq, 1), F32),
                        pltpu.VMEM((tq, WIDTH), F32)],
    )
    return pl.pallas_call(
        functools.partial(_fox_prompt_kernel, tq=tq),
        grid_spec=grid_spec,
        out_shape=jax.ShapeDtypeStruct((t, WIDTH), BF16),
        compiler_params=_cparams(("arbitrary",)),
        name="fox_prompt",
    )(qi_tab, ki_tab, qs, kb, vb, fg, fgt)


def _split3(x):
    hi = x.astype(BF16).astype(F32)
    r1 = x - hi
    mid = r1.astype(BF16).astype(F32)
    lo = (r1 - mid).astype(BF16).astype(F32)
    return [hi, mid, lo]


def _fox_sample_kernel(pt_ref, qbd_ref, knew_ref, vnew_ref, lfnew_ref, *refs, pps, n_tok):
    k_refs = refs[0:pps]
    v_refs = refs[pps:2 * pps]
    lf_refs = refs[2 * pps:3 * pps]
    o_ref, m_scr, l_scr, acc_scr, carry_scr = refs[3 * pps:]
    del pt_ref
    s_id = pl.program_id(1)
    n_rows = n_tok * C_HEADS

    @pl.when(s_id == 0)
    def _():
        m_scr[...] = jnp.full_like(m_scr, -jnp.inf)
        l_scr[...] = jnp.zeros_like(l_scr)
        acc_scr[...] = jnp.zeros_like(acc_scr)
        carry_scr[...] = jnp.zeros_like(carry_scr)

    qbd = qbd_ref[...]
    r = lax.broadcasted_iota(jnp.int32, (CHUNK, CHUNK), 0)
    c = lax.broadcasted_iota(jnp.int32, (CHUNK, CHUNK), 1)
    tri_u = jnp.where(r <= c, 1.0, 0.0).astype(BF16)

    def cumsum_pages(lfs):
        stack = jnp.concatenate([part for lf in lfs for part in _split3(lf)], axis=0).astype(BF16)
        fl = jnp.dot(stack, tri_u, preferred_element_type=F32)
        return [fl[24 * i:24 * i + 8] + fl[24 * i + 8:24 * i + 16] + fl[24 * i + 16:24 * i + 24]
                for i in range(len(lfs))]

    def attend(ks, vs, f_locals, mask):
        carry = carry_scr[...]
        scores = []
        for k, f_loc in zip(ks, f_locals):
            f_page = f_loc + carry
            carry = carry + jnp.broadcast_to(f_loc[:, CHUNK - 1:CHUNK], f_loc.shape)
            s = lax.dot_general(qbd, k.astype(BF16), NT_DIMS, preferred_element_type=F32)
            s = s - jnp.concatenate([f_page] * n_tok, axis=0)
            if mask is not None:
                s = jnp.where(mask, s, -jnp.inf)
            scores.append(s)
        carry_scr[...] = carry
        m_old = m_scr[...]
        m_new = m_old
        for s in scores:
            m_new = jnp.maximum(m_new, jnp.max(s, axis=1, keepdims=True))
        alpha = jnp.exp(m_old - m_new)
        l_new = alpha * l_scr[...]
        acc = alpha * acc_scr[...]
        for s, v in zip(scores, vs):
            p = jnp.exp(s - m_new)
            l_new = l_new + jnp.sum(p, axis=1, keepdims=True)
            acc = acc + jnp.dot(p.astype(BF16), v.astype(BF16), preferred_element_type=F32)
        m_scr[...] = m_new
        l_scr[...] = l_new
        acc_scr[...] = acc

    attend([kr[...] for kr in k_refs], [vr[...] for vr in v_refs],
           cumsum_pages([lr[...] for lr in lf_refs]), None)

    @pl.when(s_id == pl.num_programs(1) - 1)
    def _():
        row = lax.broadcasted_iota(jnp.int32, (n_rows, CHUNK), 0)
        key = lax.broadcasted_iota(jnp.int32, (n_rows, CHUNK), 1)
        attend([knew_ref[...]], [vnew_ref[...]], cumsum_pages([lfnew_ref[...]]), key <= row // C_HEADS)
        out = acc_scr[...] / l_scr[...]
        orow = lax.broadcasted_iota(jnp.int32, (n_rows, WIDTH), 0)
        olane = lax.broadcasted_iota(jnp.int32, (n_rows, WIDTH), 1)
        out = jnp.where(olane // C_HD == orow % C_HEADS, out, 0.0).astype(BF16)
        srow = lax.broadcasted_iota(jnp.int32, (8, n_rows), 0)
        scol = lax.broadcasted_iota(jnp.int32, (8, n_rows), 1)
        sel = jnp.where(scol // C_HEADS == srow, 1.0, 0.0).astype(BF16)
        o_ref[...] = jnp.dot(sel, out, preferred_element_type=F32).astype(BF16)


def _fox_sample(layer, page_table, qbd, knew, vnew, lfnew_t, cache_k, cache_v, cache_lft, n_tok):
    nb, n_pages = page_table.shape
    pps = min(8, n_pages)
    n_rows = n_tok * C_HEADS

    def page_spec(j, last):
        return pl.BlockSpec((None, None) + last,
                            lambda b, s, pt, j=j: (layer, pt[b, s * pps + j], 0, 0))

    per_b = lambda shape: pl.BlockSpec((None,) + shape, lambda b, s, pt: (b, 0, 0))
    in_specs = [per_b((n_rows, WIDTH)), per_b((CHUNK, WIDTH)), per_b((CHUNK, WIDTH)), per_b((C_HEADS, CHUNK))]
    in_specs += [page_spec(j, (CHUNK, WIDTH)) for j in range(pps)]
    in_specs += [page_spec(j, (CHUNK, WIDTH)) for j in range(pps)]
    in_specs += [page_spec(j, (C_HEADS, CHUNK)) for j in range(pps)]
    grid_spec = pltpu.PrefetchScalarGridSpec(
        num_scalar_prefetch=1,
        grid=(nb, n_pages // pps),
        in_specs=in_specs,
        out_specs=per_b((8, WIDTH)),
        scratch_shapes=[pltpu.VMEM((n_rows, 1), F32), pltpu.VMEM((n_rows, 1), F32),
                        pltpu.VMEM((n_rows, WIDTH), F32), pltpu.VMEM((C_HEADS, CHUNK), F32)],
    )
    return pl.pallas_call(
        functools.partial(_fox_sample_kernel, pps=pps, n_tok=n_tok),
        grid_spec=grid_spec,
        out_shape=jax.ShapeDtypeStruct((nb, 8, WIDTH), BF16),
        compiler_params=_cparams(("parallel", "arbitrary")),
        name="fox_sample",
    )(page_table, qbd, knew, vnew, lfnew_t, *([cache_k] * pps), *([cache_v] * pps), *([cache_lft] * pps))


def _merge_kernel(x_ref, ya_ref, yb_ref, yc_ref, zg_ref, wa_ref, wb_ref, wc_ref, wo_ref, g_ref, o_ref):
    d = x_ref.shape[1]
    merged = None
    for i, (y_ref, w_ref) in enumerate(((ya_ref, wa_ref), (yb_ref, wb_ref), (yc_ref, wc_ref))):
        term = jax.nn.sigmoid(zg_ref[:, i * d:(i + 1) * d]) * jnp.dot(y_ref[...], w_ref[...],
                                                                      preferred_element_type=F32)
        merged = term if merged is None else merged + term
    t = jnp.dot(merged.astype(BF16), wo_ref[...], preferred_element_type=F32)
    o_ref[...] = x_ref[...] + _rms(t) * g_ref[...]


def _merge(x, ya, yb, yc, zg, wa, wb, wc, wo, g):
    rows, d = x.shape
    tm = 256 if rows % 256 == 0 else 128
    row = lambda n: pl.BlockSpec((tm, n), lambda i: (i, 0))
    return pl.pallas_call(
        _merge_kernel,
        grid=(rows // tm,),
        in_specs=[row(d), row(WIDTH), row(WIDTH), row(WIDTH), row(zg.shape[1]),
                  _const_spec(wa.shape), _const_spec(wb.shape), _const_spec(wc.shape), _const_spec(wo.shape),
                  _const_spec((1, d))],
        out_specs=row(d),
        out_shape=jax.ShapeDtypeStruct((rows, d), F32),
        compiler_params=_cparams(("parallel",)),
        name="merge",
    )(x, ya, yb, yc, zg, wa, wb, wc, wo, g)


def _ffn_kernel(x_ref, gpre_ref, wg_ref, wu_ref, wd_ref, gpost_ref, o_ref):
    x = x_ref[...]
    h = (_rms(x) * gpre_ref[...]).astype(BF16)
    gate = jnp.dot(h, wg_ref[...], preferred_element_type=F32)
    up = jnp.dot(h, wu_ref[...], preferred_element_type=F32)
    act = (gate * jax.nn.sigmoid(gate) * up).astype(BF16)
    f = jnp.dot(act, wd_ref[...], preferred_element_type=F32)
    o_ref[...] = x + _rms(f) * gpost_ref[...]


def _ffn(x, gpre, wg, wu, wd, gpost):
    rows, d = x.shape
    tm = 256 if rows % 256 == 0 else 128
    row = pl.BlockSpec((tm, d), lambda i: (i, 0))
    return pl.pallas_call(
        _ffn_kernel,
        grid=(rows // tm,),
        in_specs=[row, _const_spec((1, d)), _const_spec(wg.shape), _const_spec(wu.shape), _const_spec(wd.shape),
                  _const_spec((1, d))],
        out_specs=row,
        out_shape=jax.ShapeDtypeStruct((rows, d), F32),
        compiler_params=_cparams(("parallel",)),
        name="ffn",
    )(x, gpre, wg, wu, wd, gpost)


def _prep_w_in(w):
    sizes = (WIDTH, WIDTH, WIDTH, WIDTH, WIDTH, B_HEADS, B_HEADS, WIDTH, WIDTH, WIDTH, C_HEADS)
    offs = [0]
    for sz in sizes:
        offs.append(offs[-1] + sz)
    seg = lambda i: w[:, offs[i]:offs[i + 1]]
    small = jnp.concatenate([seg(5), seg(6), seg(10)], axis=1)
    small = jnp.pad(small, ((0, 0), (0, SMALL - small.shape[1])))
    main = [seg(i) for i in (0, 1, 2, 3, 4, 7, 8, 9)]
    return jnp.concatenate(main + [w[:, offs[-1]:], small], axis=1).astype(BF16)


def _layer_params(l, g_pre_mix, w_in, a_w_s, a_b_s, a_g_v, b_conv_w, b_conv_b, b_b_i, b_b_f, c_b_f, w_out_a,
                  w_out_b, w_out_c, w_o, g_post_mix, g_pre_ffn, w_gate, w_up, w_down, g_post_ffn, n_batch, n_tok):
    p = {}
    p['g_pre_mix'] = g_pre_mix[l][None, :]
    p['w_in'] = _prep_w_in(w_in[l])
    tril = jnp.tril(jnp.ones((CHUNK, CHUNK), bool))
    p['a_w_prompt'] = jnp.where(tril, a_w_s[l], 0).astype(BF16)
    p['a_bias_prompt'] = jnp.repeat(a_b_s[l].T, LANES, axis=1)
    w_small = jnp.where(tril[:n_tok, :n_tok], a_w_s[l][:, :n_tok, :n_tok], 0)
    eye = jnp.eye(n_batch, dtype=F32)
    p['a_w_sample'] = jnp.einsum('ab,gts->gatbs', eye, w_small).reshape(
        A_GROUPS, n_batch * n_tok, n_batch * n_tok).astype(BF16)
    p['a_bias_sample'] = jnp.tile(jnp.repeat(a_b_s[l][:, :n_tok].T, LANES, axis=1), (n_batch, 1))
    p['a_g_v'] = a_g_v[l][None, :]
    p['conv_w'] = b_conv_w[l]
    p['conv_b'] = b_conv_b[l][None, :]
    gate_bias = jnp.concatenate([b_b_i[l], b_b_f[l], c_b_f[l]])
    p['gate_bias'] = jnp.pad(gate_bias, (0, SMALL - gate_bias.shape[0]))[None, :]
    p['w_out_a'] = w_out_a[l].astype(BF16)
    p['w_out_b'] = w_out_b[l].astype(BF16)
    p['w_out_c'] = w_out_c[l].astype(BF16)
    p['w_o'] = w_o[l].astype(BF16)
    p['g_post_mix'] = g_post_mix[l][None, :]
    p['g_pre_ffn'] = g_pre_ffn[l][None, :]
    p['w_gate'] = w_gate[l].astype(BF16)
    p['w_up'] = w_up[l].astype(BF16)
    p['w_down'] = w_down[l].astype(BF16)
    p['g_post_ffn'] = g_post_ffn[l][None, :]
    return p


def _finish_layer(x, p, ya, yb, yc, zg):
    x = _merge(x, ya, yb, yc, zg, p['w_out_a'], p['w_out_b'], p['w_out_c'], p['w_o'], p['g_post_mix'])
    return _ffn(x, p['g_pre_ffn'], p['w_gate'], p['w_up'], p['w_down'], p['g_post_ffn'])


def _prompt_layer(x, p):
    zg, au, av, bqk, bv, bo, ck, cv, small, qs, kb, vb = _inproj(x, p['g_pre_mix'], p['w_in'])
    (ya,) = _mixa(au, av, p['a_g_v'], p['a_w_prompt'], p['a_bias_prompt'], emit_vn=False)
    g, cb, fg, gt, cbt, fgt = _gates(small, p['gate_bias'])
    yb, cn, m = _mlstm_scan(bqk, bv, bo, g, cb, gt, cbt, p['conv_w'], p['conv_b'])
    yc = _fox_prompt(qs, kb, vb, fg, fgt)
    x = _finish_layer(x, p, ya, yb, yc, zg)
    t = x.shape[0]
    state = (ck.reshape(1, t, C_HEADS, C_HD), cv.reshape(1, t, C_HEADS, C_HD), g[:, 8:16].reshape(1, t, C_HEADS),
             bqk[t - (B_CONV - 1):][None],
             cn[:, :, :B_DV].reshape(1, B_HEADS, B_DK, B_DV), cn[:, :, B_DV].reshape(1, B_HEADS, B_DK),
             m[0, :B_HEADS][None])
    return x, state


def _sample_layer(x, p, l, nb, nt, page_table, cache_k, cache_v, cache_lft, conv_state, c_state, n_state, m_state):
    zg, au, av, bqk, bv, bo, ck, cv, small, qs, kb, vb = _inproj(x, p['g_pre_mix'], p['w_in'])
    ya, vn = _mixa(au, av, p['a_g_v'], p['a_w_sample'], p['a_bias_sample'], emit_vn=True)

    pad_rows = lambda a: jnp.pad(a.reshape(nb, nt, -1), ((0, 0), (0, CHUNK - nt), (0, 0))).reshape(nb * CHUNK, -1)
    g, cb, _, gt, cbt, _ = _gates(pad_rows(small), p['gate_bias'])
    conv_in = jnp.pad(conv_state, ((0, 0), (8 - (B_CONV - 1), 0), (0, 0)))
    cn_in = jnp.concatenate([c_state.reshape(nb, 2, LANES, B_DV), n_state.reshape(nb, 2, LANES, 1),
                             jnp.zeros((nb, 2, LANES, LANES - 1), F32)], axis=-1)
    m_in = jnp.pad(m_state[:, None, :], ((0, 0), (0, 7), (0, LANES - B_HEADS)))
    yb_pad, cn, m = _mlstm_batch(pad_rows(bqk), pad_rows(bv), pad_rows(bo), g, cb, gt, cbt, p['conv_w'],
                                 p['conv_b'], conv_in, cn_in, m_in, valid=nt)
    yb = yb_pad.reshape(nb, CHUNK, WIDTH)[:, :nt].reshape(nb * nt, WIDTH)

    clf = g.reshape(nb, CHUNK, SMALL)[:, :nt, 2 * B_HEADS:2 * B_HEADS + C_HEADS].reshape(nb * nt, C_HEADS)
    q4 = qs.reshape(nb, nt, 1, C_HEADS, C_HD)
    head_eye = jnp.eye(C_HEADS, dtype=BF16)[None, None, :, :, None]
    qbd = (q4 * head_eye).reshape(nb, nt * C_HEADS, WIDTH)
    pad_keys = lambda a: jnp.pad(a.reshape(nb, nt, WIDTH), ((0, 0), (0, CHUNK - nt), (0, 0)))
    lfnew_t = jnp.pad(jnp.swapaxes(clf.reshape(nb, nt, C_HEADS), 1, 2), ((0, 0), (0, 0), (0, CHUNK - nt)))
    yc8 = _fox_sample(l, page_table, qbd, pad_keys(ck), pad_keys(cv), lfnew_t, cache_k, cache_v, cache_lft, nt)
    yc = yc8[:, :nt].reshape(nb * nt, WIDTH)

    x = _finish_layer(x, p, ya, yb, yc, zg)
    state = (ck.reshape(nb, nt, C_HEADS, C_HD), cv.reshape(nb, nt, C_HEADS, C_HD), clf.reshape(nb, nt, C_HEADS),
             bqk.reshape(nb, nt, WIDTH)[:, nt - (B_CONV - 1):],
             cn[..., :B_DV].reshape(nb, B_HEADS, B_DK, B_DV), cn[..., B_DV].reshape(nb, B_HEADS, B_DK),
             m[:, 0, :B_HEADS], vn.reshape(nb, nt, WIDTH))
    return x, state


def kernel(x_prompt, x_sample, cache_k, cache_v, cache_logf, page_table, state_mlstm_conv, state_mlstm_c, state_mlstm_n, state_mlstm_m, g_pre_mix, w_in, a_w_s, a_b_s, a_g_v, b_conv_w, b_conv_b, b_b_i, b_b_f, c_b_f, w_out_a, w_out_b, w_out_c, w_o, g_post_mix, g_pre_ffn, w_gate, w_up, w_down, g_post_ffn):
    depth = w_in.shape[0]
    bp, t, d = x_prompt.shape
    assert bp == 1, "the prompt group is a single sequence"
    nb, nt, _ = x_sample.shape
    n_pool, page = cache_k.shape[1], cache_k.shape[2]
    assert page == CHUNK and nb * nt == CHUNK and t % CHUNK == 0
    ck4 = cache_k.reshape(depth, n_pool, page, WIDTH)
    cv4 = cache_v.reshape(depth, n_pool, page, WIDTH)
    clft = jnp.swapaxes(cache_logf, 2, 3)
    xp = x_prompt.reshape(t, d)
    xs = x_sample.reshape(nb * nt, d)
    outs_p, outs_s = [], []
    for l in range(depth):
        p = _layer_params(l, g_pre_mix, w_in, a_w_s, a_b_s, a_g_v, b_conv_w, b_conv_b, b_b_i, b_b_f, c_b_f,
                          w_out_a, w_out_b, w_out_c, w_o, g_post_mix, g_pre_ffn, w_gate, w_up, w_down, g_post_ffn,
                          nb, nt)
        xp, st_p = _prompt_layer(xp, p)
        xs, st_s = _sample_layer(xs, p, l, nb, nt, page_table, ck4, cv4, clft, state_mlstm_conv[l],
                                 state_mlstm_c[l], state_mlstm_n[l], state_mlstm_m[l])
        outs_p.append(st_p)
        outs_s.append(st_s)
    stack = lambda outs, i: jnp.stack([o[i] for o in outs], axis=0)
    return ((xp.reshape(1, t, d), xs.reshape(nb, nt, d))
            + tuple(stack(outs_p, i) for i in range(7))
            + tuple(stack(outs_s, i) for i in range(8)))
```
